```python
import jax, jax.numpy as jnp
from jax import lax
import numpy as np

D_MODEL = 2048
BATCH = 4
SEQ = 4096
DEPTH = 1
DEC_BATCH = 32
DEC_SEQ = 32
PAST_LEN = 2048

CHUNK = 64
N_MEM = 256
POOL_WINDOWS = (2, 4, 8, 16)
POOL_GROUPS = 4
POOL_WIDTH = D_MODEL // 4
POOL_GDIM = POOL_WIDTH // POOL_GROUPS
POOL_STATE = 15
GLA_HEADS = 4
GLA_VDIM = D_MODEL // 2
GLA_KDIM = GLA_VDIM // 2
GLA_HK = GLA_KDIM // GLA_HEADS
GLA_HV = GLA_VDIM // GLA_HEADS
GLA_GATE_RANK = 16
GLA_TAU = 16.0
MEM_HEADS = 4
MEM_WIDTH = D_MODEL // 4
MEM_HD = MEM_WIDTH // MEM_HEADS
D_FF = 5632
CONV_W = 3
N_BRANCH = 3
EPS = 1e-6
IN_SIZES = (POOL_WIDTH, GLA_KDIM, GLA_KDIM, GLA_VDIM, GLA_VDIM, GLA_GATE_RANK, MEM_WIDTH)
IN_WIDTH = POOL_WIDTH + 2 * GLA_KDIM + 2 * GLA_VDIM + GLA_GATE_RANK + MEM_WIDTH

kernel_name = 'hybrid_pool_gla_memory_streaming_step'


def rmsnorm(x, g):
    xf = x.astype(jnp.float32)
    y = xf * lax.rsqrt(jnp.mean(xf * xf, axis=-1, keepdims=True) + EPS)
    return (y * g.astype(jnp.float32)).astype(x.dtype)


def pool_mixer(u, hist, start_pos, w_pool, s_pool):
    B, T, P = u.shape
    full = jnp.concatenate([hist, u], axis=1)
    cs = jnp.cumsum(full.astype(jnp.float32), axis=1)
    cs = jnp.pad(cs, ((0, 0), (1, 0), (0, 0)))
    L = POOL_STATE
    pos = start_pos + jnp.arange(T)
    end = cs[:, L + 1:L + 1 + T]
    outs = []
    for g, w in enumerate(POOL_WINDOWS):
        sl = slice(g * POOL_GDIM, (g + 1) * POOL_GDIM)
        begin = cs[:, L + 1 - w:L + 1 - w + T, sl]
        cnt = jnp.minimum(pos + 1, w).astype(jnp.float32)[None, :, None]
        mean = (end[..., sl] - begin) / cnt
        outs.append(mean.astype(u.dtype) - u[..., sl])
    d = jnp.stack(outs, axis=2)
    y = jnp.einsum('btgc,gcd->btgd', d, w_pool).reshape(B, T, P) * s_pool
    return y, full[:, -POOL_STATE:]


def gla_chunked(q, k, v, log_a, s0):
    B, T, H, dk = q.shape
    dv = v.shape[-1]
    C = CHUNK if T % CHUNK == 0 else T
    N = T // C
    f32 = jnp.float32
    qc = q.astype(f32).reshape(B, N, C, H, dk) * (dk ** -0.5)
    kc = k.astype(f32).reshape(B, N, C, H, dk)
    vc = v.astype(f32).reshape(B, N, C, H, dv)
    b = jnp.cumsum(log_a.reshape(B, N, C, H, dk), axis=2)
    b_last = b[:, :, -1]
    q_e = qc * jnp.exp(b)
    k_e = kc * jnp.exp(-b)
    k_tail = kc * jnp.exp(b_last[:, :, None] - b)
    causal = jnp.tril(jnp.ones((C, C), dtype=bool))
    att = jnp.where(causal, jnp.einsum('bnihk,bnjhk->bnhij', q_e, k_e), 0.0)
    o_intra = jnp.einsum('bnhij,bnjhv->bnihv', att, vc)
    kv = jnp.einsum('bnjhk,bnjhv->nbhkv', k_tail, vc)
    decay = jnp.exp(b_last).transpose(1, 0, 2, 3)

    def step(s, inp):
        dec, kv_n = inp
        return dec[..., None] * s + kv_n, s

    s_final, s_starts = lax.scan(step, s0.astype(f32), (decay, kv))
    o_inter = jnp.einsum('bnihk,nbhkv->bnihv', q_e, s_starts)
    o = (o_intra + o_inter).reshape(B, T, H, dv).astype(q.dtype)
    return o, s_final


def memory_kv(mem, g_mem, w_mem_k, w_mem_v):
    B = mem.shape[0]
    m = rmsnorm(mem, g_mem)
    k = (m @ w_mem_k).reshape(B, N_MEM, MEM_HEADS, MEM_HD)
    v = (m @ w_mem_v).reshape(B, N_MEM, MEM_HEADS, MEM_HD)
    return k, v


def memory_attend(q, k, v):
    s = jnp.einsum('bthd,bmhd->bhtm', q, k).astype(jnp.float32) * (MEM_HD ** -0.5)
    p = jax.nn.softmax(s, axis=-1).astype(v.dtype)
    return jnp.einsum('bhtm,bmhd->bthd', p, v)


def conv_ffn(x, hist, w_up, w_dw, b_dw, w_down):
    T = x.shape[1]
    a = x @ w_up
    full = jnp.concatenate([hist, a], axis=1)
    c = full[:, 0:T] * w_dw[0]
    for j in range(1, CONV_W):
        c = c + full[:, j:j + T] * w_dw[j]
    c = c + b_dw
    g, val = jnp.split(c, 2, axis=-1)
    y = (jax.nn.gelu(g) * val) @ w_down
    return y, full[:, -(CONV_W - 1):]


def layer(x, mem_k, mem_v, pool_hist, gla_s, conv_hist, start_pos,
          g_pre_mix, w_in, w_pool, s_pool, w_gla_gate, b_gla_gate, g_gla_head,
          w_gate, b_gate, w_b_pool, w_b_gla, w_b_mem, w_o, g_post_mix,
          g_pre_ffn, w_up, w_dw, b_dw, w_down, g_post_ffn):
    B, T, _ = x.shape
    h = rmsnorm(x, g_pre_mix)
    z = h @ w_in
    u, q, k, v, r, a_lr, mq = jnp.split(z, np.cumsum(IN_SIZES)[:-1].tolist(), axis=-1)
    y_pool, new_pool = pool_mixer(u, pool_hist, start_pos, w_pool, s_pool)
    log_a = jax.nn.log_sigmoid((a_lr @ w_gla_gate + b_gla_gate).astype(jnp.float32)) / GLA_TAU
    o, new_s = gla_chunked(q.reshape(B, T, GLA_HEADS, GLA_HK), k.reshape(B, T, GLA_HEADS, GLA_HK),
                           v.reshape(B, T, GLA_HEADS, GLA_HV),
                           log_a.reshape(B, T, GLA_HEADS, GLA_HK), gla_s)
    o = rmsnorm(o, g_gla_head).reshape(B, T, GLA_VDIM) * jax.nn.silu(r)
    m = memory_attend(mq.reshape(B, T, MEM_HEADS, MEM_HD), mem_k, mem_v).reshape(B, T, MEM_WIDTH)
    gates = jax.nn.sigmoid(h @ w_gate + b_gate).reshape(B, T, N_BRANCH, D_MODEL)
    merged = (gates[:, :, 0] * (y_pool @ w_b_pool) + gates[:, :, 1] * (o @ w_b_gla)
              + gates[:, :, 2] * (m @ w_b_mem))
    x = x + rmsnorm(merged @ w_o, g_post_mix)
    f, new_conv = conv_ffn(rmsnorm(x, g_pre_ffn), conv_hist, w_up, w_dw, b_dw, w_down)
    x = x + rmsnorm(f, g_post_ffn)
    return x, new_pool, new_s, new_conv


def setup_inputs(seed: int = 0) -> dict:
    key = jax.random.key(seed)
    ks = iter(jax.random.split(key, 40))

    def nrm(shape, scale=1.0):
        return jax.random.normal(next(ks), shape, jnp.float32) * scale

    def gain(shape):
        return 1.0 + nrm(shape, 0.02)

    L = DEPTH
    F2 = 2 * D_FF
    return {
        'x_prompt': nrm((BATCH, SEQ, D_MODEL)),
        'x_sample': nrm((DEC_BATCH, DEC_SEQ, D_MODEL)),
        'mem_prompt': nrm((BATCH, N_MEM, D_MODEL)),
        'cache_pool': nrm((L, DEC_BATCH, POOL_STATE, POOL_WIDTH)),
        'state_gla': nrm((L, DEC_BATCH, GLA_HEADS, GLA_HK, GLA_HV)),
        'cache_ffn_conv': nrm((L, DEC_BATCH, CONV_W - 1, F2)),
        'cache_mem_k': nrm((L, DEC_BATCH, N_MEM, MEM_HEADS, MEM_HD)),
        'cache_mem_v': nrm((L, DEC_BATCH, N_MEM, MEM_HEADS, MEM_HD)),
        'g_mem': gain((L, D_MODEL)),
        'w_mem_k': nrm((L, D_MODEL, MEM_WIDTH), D_MODEL ** -0.5),
        'w_mem_v': nrm((L, D_MODEL, MEM_WIDTH), D_MODEL ** -0.5),
        'g_pre_mix': gain((L, D_MODEL)),
        'w_in': nrm((L, D_MODEL, IN_WIDTH), D_MODEL ** -0.5),
        'w_pool': nrm((L, POOL_GROUPS, POOL_GDIM, POOL_GDIM), POOL_GDIM ** -0.5),
        's_pool': gain((L, POOL_WIDTH)),
        'w_gla_gate': nrm((L, GLA_GATE_RANK, GLA_KDIM), GLA_GATE_RANK ** -0.5),
        'b_gla_gate': nrm((L, GLA_KDIM), 0.1),
        'g_gla_head': gain((L, GLA_HEADS, GLA_HV)),
        'w_gate': nrm((L, D_MODEL, N_BRANCH * D_MODEL), D_MODEL ** -0.5),
        'b_gate': nrm((L, N_BRANCH * D_MODEL), 0.01),
        'w_b_pool': nrm((L, POOL_WIDTH, D_MODEL), POOL_WIDTH ** -0.5),
        'w_b_gla': nrm((L, GLA_VDIM, D_MODEL), GLA_VDIM ** -0.5),
        'w_b_mem': nrm((L, MEM_WIDTH, D_MODEL), MEM_WIDTH ** -0.5),
        'w_o': nrm((L, D_MODEL, D_MODEL), D_MODEL ** -0.5),
        'g_post_mix': gain((L, D_MODEL)),
        'g_pre_ffn': gain((L, D_MODEL)),
        'w_up': nrm((L, D_MODEL, F2), D_MODEL ** -0.5),
        'w_dw': nrm((L, CONV_W, F2), CONV_W ** -0.5),
        'b_dw': nrm((L, F2), 0.01),
        'w_down': nrm((L, D_FF, D_MODEL), D_FF ** -0.5),
        'g_post_ffn': gain((L, D_MODEL)),
    }


def reference(x_prompt, x_sample, mem_prompt, cache_pool, state_gla, cache_ffn_conv,
              cache_mem_k, cache_mem_v, g_mem, w_mem_k, w_mem_v, g_pre_mix, w_in, w_pool,
              s_pool, w_gla_gate, b_gla_gate, g_gla_head, w_gate, b_gate, w_b_pool, w_b_gla,
              w_b_mem, w_o, g_post_mix, g_pre_ffn, w_up, w_dw, b_dw, w_down, g_post_ffn):
    yp = x_prompt
    ys = x_sample
    bp = x_prompt.shape[0]
    pool_p, pool_s, gla_p, gla_s, conv_p, conv_s, memk_p, memv_p = [], [], [], [], [], [], [], []
    for l in range(DEPTH):
        lw = (g_pre_mix[l], w_in[l], w_pool[l], s_pool[l], w_gla_gate[l], b_gla_gate[l],
              g_gla_head[l], w_gate[l], b_gate[l], w_b_pool[l], w_b_gla[l], w_b_mem[l], w_o[l],
              g_post_mix[l], g_pre_ffn[l], w_up[l], w_dw[l], b_dw[l], w_down[l], g_post_ffn[l])
        mk, mv = memory_kv(mem_prompt, g_mem[l], w_mem_k[l], w_mem_v[l])
        zp = jnp.zeros((bp, POOL_STATE, POOL_WIDTH), x_prompt.dtype)
        zs = jnp.zeros((bp, GLA_HEADS, GLA_HK, GLA_HV), jnp.float32)
        zc = jnp.zeros((bp, CONV_W - 1, 2 * D_FF), x_prompt.dtype)
        yp, np_, ns_, nc_ = layer(yp, mk, mv, zp, zs, zc, 0, *lw)
        pool_p.append(np_)
        gla_p.append(ns_)
        conv_p.append(nc_)
        memk_p.append(mk)
        memv_p.append(mv)
        ys, np2, ns2, nc2 = layer(ys, cache_mem_k[l], cache_mem_v[l], cache_pool[l], state_gla[l],
                                  cache_ffn_conv[l], PAST_LEN, *lw)
        pool_s.append(np2)
        gla_s.append(ns2)
        conv_s.append(nc2)
    return (yp, ys, jnp.stack(pool_p), jnp.stack(pool_s), jnp.stack(gla_p), jnp.stack(gla_s),
            jnp.stack(conv_p), jnp.stack(conv_s), jnp.stack(memk_p), jnp.stack(memv_p))
```

```python
import functools

import jax
import jax.numpy as jnp
from jax import lax
from jax.experimental import pallas as pl
from jax.experimental.pallas import tpu as pltpu

F32 = jnp.float32
BF16 = jnp.bfloat16

D_MODEL = 2048
CHUNK = 64
N_MEM = 256
POOL_WINDOWS = (2, 4, 8, 16)
POOL_WIDTH = 512
POOL_GDIM = 128
POOL_STATE = 15
GLA_HEADS = 4
GLA_VDIM = 1024
GLA_KDIM = 512
GLA_HK = 128
GLA_HV = 256
GLA_GATE_RANK = 16
GLA_TAU = 16.0
MEM_HEADS = 4
MEM_WIDTH = 512
MEM_HD = 128
D_FF = 5632
CONV_W = 3
EPS = 1e-6
PAST_LEN = 2048

ZQ, ZK, ZV, ZR, ZM = 0, 512, 1024, 2048, 3072
Z_WIDTH = 3584
BR_POOL, BR_GLA, BR_MEM = 0, 512, 1536

LANES = 128
SUBLANES = 8
VMEM_LIMIT = 56 * 1024 * 1024


def _rms(x, g):
    return x * lax.rsqrt(jnp.mean(x * x, axis=-1, keepdims=True) + EPS) * g


def _sigmoid(x):
    return 1.0 / (1.0 + jnp.exp(-x))


def _dot(a, b):
    return jnp.dot(a, b, preferred_element_type=F32)


def _params(*sem):
    return pltpu.CompilerParams(dimension_semantics=sem, vmem_limit_bytes=VMEM_LIMIT)


def _resident(shape):
    nd = len(shape)
    return pl.BlockSpec(shape, lambda *_: (0,) * nd, pipeline_mode=pl.Buffered(1))


def _memkv_kernel(m_ref, g_ref, w_ref, k_ref, v_ref):
    h = _rms(m_ref[...], g_ref[...]).astype(BF16)
    kv = _dot(h, w_ref[...])
    k_ref[...] = kv[:, :MEM_WIDTH]
    v_ref[...] = kv[:, MEM_WIDTH:]


def _mem_kv(mem, g, w_kv):
    n = mem.shape[0]
    tm = 512
    return pl.pallas_call(
        _memkv_kernel,
        grid=(n // tm,),
        in_specs=[pl.BlockSpec((tm, D_MODEL), lambda i: (i, 0)),
                  _resident((1, D_MODEL)),
                  _resident((D_MODEL, 2 * MEM_WIDTH))],
        out_specs=[pl.BlockSpec((tm, MEM_WIDTH), lambda i: (i, 0)),
                   pl.BlockSpec((tm, MEM_WIDTH), lambda i: (i, 0))],
        out_shape=[jax.ShapeDtypeStruct((n, MEM_WIDTH), F32)] * 2,
        compiler_params=_params("arbitrary"),
        name="mem_kv",
    )(mem, g, w_kv)


_INPROJ_TN = 512


def _inproj_kernel(x_ref, g_ref, w_ref, walr_ref, u_ref, z_ref, alr_ref):
    h = _rms(x_ref[...], g_ref[...]).astype(BF16)
    u_ref[...] = _dot(h, w_ref[:, 0:POOL_WIDTH])
    for j in range(Z_WIDTH // _INPROJ_TN):
        c0 = j * _INPROJ_TN
        z_ref[:, c0:c0 + _INPROJ_TN] = _dot(
            h, w_ref[:, POOL_WIDTH + c0:POOL_WIDTH + c0 + _INPROJ_TN]).astype(BF16)
    alr_ref[...] = _dot(h, walr_ref[...])


def _in_proj(x, g, w_main, w_alr):
    n = x.shape[0]
    tm = 512
    return pl.pallas_call(
        _inproj_kernel,
        grid=(n // tm,),
        in_specs=[pl.BlockSpec((tm, D_MODEL), lambda i: (i, 0)),
                  _resident((1, D_MODEL)),
                  _resident((D_MODEL, POOL_WIDTH + Z_WIDTH)),
                  _resident((D_MODEL, LANES))],
        out_specs=[pl.BlockSpec((tm, POOL_WIDTH), lambda i: (i, 0)),
                   pl.BlockSpec((tm, Z_WIDTH), lambda i: (i, 0)),
                   pl.BlockSpec((tm, LANES), lambda i: (i, 0))],
        out_shape=[jax.ShapeDtypeStruct((n, POOL_WIDTH), F32),
                   jax.ShapeDtypeStruct((n, Z_WIDTH), BF16),
                   jax.ShapeDtypeStruct((n, LANES), F32)],
        compiler_params=_params("arbitrary"),
        name="in_proj",
    )(x, g, w_main, w_alr)


def _mixer_kernel(u_ref, z_ref, alr_ref, mk_ref, mv_ref, ph_ref, s0_ref,
                  wpool_ref, spool_ref, wgg_ref, bgg_ref, ghead_ref,
                  br_ref, pool_out_ref, s_out_ref,
                  ubuf, s_scr, la_scr, *, tm, chunk, start_pos):
    t = pl.program_id(1)
    last_t = pl.num_programs(1) - 1
    hist = POOL_STATE + 1

    @pl.when(t == 0)
    def _():
        ubuf[0:1, :] = jnp.zeros((1, POOL_WIDTH), F32)
        ubuf[1:hist, :] = ph_ref[0]
        s_scr[...] = s0_ref[0]

    u = u_ref[...]
    ubuf[hist:hist + tm, :] = u
    pos = start_pos + t * tm + lax.broadcasted_iota(jnp.int32, (tm, 1), 0)
    for g, w in enumerate(POOL_WINDOWS):
        sl = slice(g * POOL_GDIM, (g + 1) * POOL_GDIM)
        ug = u[:, sl]
        wsum = ug
        for i in range(1, w):
            wsum = wsum + ubuf[hist - i:hist - i + tm, sl]
        cnt = jnp.minimum(pos + 1, w).astype(F32)
        d = (wsum / cnt - ug).astype(BF16)
        y = _dot(d, wpool_ref[g]) * spool_ref[:, sl]
        br_ref[:, BR_POOL + g * POOL_GDIM:BR_POOL + (g + 1) * POOL_GDIM] = y.astype(BF16)
    tail = ubuf[tm:tm + hist, :]
    ubuf[0:hist, :] = tail

    @pl.when(t == last_t)
    def _():
        pool_out_ref[0] = tail[1:hist, :]

    x_gate = _dot(alr_ref[...].astype(BF16), wgg_ref[...]) + bgg_ref[...]
    la_scr[...] = (jnp.minimum(x_gate, 0.0) - jnp.log1p(jnp.exp(-jnp.abs(x_gate)))) / GLA_TAU

    row = lax.broadcasted_iota(jnp.int32, (chunk, chunk), 0)
    col = lax.broadcasted_iota(jnp.int32, (chunk, chunk), 1)
    causal = row >= col
    tri = causal.astype(F32)
    scale = GLA_HK ** -0.5

    def chunk_body(c, carry):
        r0 = pl.multiple_of(c * chunk, chunk)
        rows = pl.ds(r0, chunk)
        for h in range(GLA_HEADS):
            hk = slice(h * GLA_HK, (h + 1) * GLA_HK)
            hv = slice(h * GLA_HV, (h + 1) * GLA_HV)
            la = la_scr[rows, hk]
            b = jnp.dot(tri, la, precision=lax.Precision.HIGHEST, preferred_element_type=F32)
            b_last = b[chunk - 1:chunk, :]
            q = z_ref[rows, ZQ + h * GLA_HK:ZQ + (h + 1) * GLA_HK].astype(F32)
            k = z_ref[rows, ZK + h * GLA_HK:ZK + (h + 1) * GLA_HK].astype(F32)
            v = z_ref[rows, ZV + h * GLA_HV:ZV + (h + 1) * GLA_HV]
            q_e = ((q * scale) * jnp.exp(b)).astype(BF16)
            k_e = (k * jnp.exp(-b)).astype(BF16)
            k_tail = (k * jnp.exp(b_last - b)).astype(BF16)
            att = lax.dot_general(q_e, k_e, (((1,), (1,)), ((), ())), preferred_element_type=F32)
            att = jnp.where(causal, att, 0.0).astype(BF16)
            s_h = s_scr[h]
            o = _dot(att, v) + _dot(q_e, s_h.astype(BF16))
            kv = lax.dot_general(k_tail, v, (((0,), (0,)), ((), ())), preferred_element_type=F32)
            decay = jnp.exp(jnp.broadcast_to(b_last, (GLA_HK, GLA_HK))).T
            s_scr[h] = jnp.concatenate([decay, decay], axis=1) * s_h + kv
            r = z_ref[rows, ZR + h * GLA_HV:ZR + (h + 1) * GLA_HV].astype(F32)
            o_n = _rms(o, ghead_ref[:, hv]) * (r * _sigmoid(r))
            br_ref[rows, BR_GLA + h * GLA_HV:BR_GLA + (h + 1) * GLA_HV] = o_n.astype(BF16)
        return carry

    lax.fori_loop(0, tm // chunk, chunk_body, 0)

    @pl.when(t == last_t)
    def _():
        s_out_ref[0] = s_scr[...]

    for h in range(MEM_HEADS):
        hd = slice(h * MEM_HD, (h + 1) * MEM_HD)
        qh = z_ref[:, ZM + h * MEM_HD:ZM + (h + 1) * MEM_HD]
        kh = mk_ref[0, :, hd].astype(BF16)
        vh = mv_ref[0, :, hd].astype(BF16)
        s = lax.dot_general(qh, kh, (((1,), (1,)), ((), ())), preferred_element_type=F32) * (MEM_HD ** -0.5)
        e = jnp.exp(s - jnp.max(s, axis=-1, keepdims=True))
        p = e / jnp.sum(e, axis=-1, keepdims=True)
        m = _dot(p.astype(BF16), vh)
        br_ref[:, BR_MEM + h * MEM_HD:BR_MEM + (h + 1) * MEM_HD] = m.astype(BF16)


def _mixer(u, z, alr, mem_k, mem_v, pool_hist, s0, wpool, spool, wgg, bgg, ghead,
           *, batch, seq, tm, chunk, start_pos):
    nt = seq // tm
    tok = lambda b, t: (b * nt + t, 0)
    per_b3 = lambda b, t: (b, 0, 0)
    per_b4 = lambda b, t: (b, 0, 0, 0)
    kern = functools.partial(_mixer_kernel, tm=tm, chunk=chunk, start_pos=start_pos)
    return pl.pallas_call(
        kern,
        grid=(batch, nt),
        in_specs=[pl.BlockSpec((tm, POOL_WIDTH), tok),
                  pl.BlockSpec((tm, Z_WIDTH), tok),
                  pl.BlockSpec((tm, LANES), tok),
                  pl.BlockSpec((1, N_MEM, MEM_WIDTH), per_b3),
                  pl.BlockSpec((1, N_MEM, MEM_WIDTH), per_b3),
                  pl.BlockSpec((1, POOL_STATE, POOL_WIDTH), per_b3),
                  pl.BlockSpec((1, GLA_HEADS, GLA_HK, GLA_HV), per_b4),
                  _resident((len(POOL_WINDOWS), POOL_GDIM, POOL_GDIM)),
                  _resident((1, POOL_WIDTH)),
                  _resident((LANES, GLA_KDIM)),
                  _resident((1, GLA_KDIM)),
                  _resident((1, GLA_VDIM))],
        out_specs=[pl.BlockSpec((tm, D_MODEL), tok),
                   pl.BlockSpec((1, POOL_STATE, POOL_WIDTH), per_b3),
                   pl.BlockSpec((1, GLA_HEADS, GLA_HK, GLA_HV), per_b4)],
        out_shape=[jax.ShapeDtypeStruct((batch * seq, D_MODEL), BF16),
                   jax.ShapeDtypeStruct((batch, POOL_STATE, POOL_WIDTH), F32),
                   jax.ShapeDtypeStruct((batch, GLA_HEADS, GLA_HK, GLA_HV), F32)],
        scratch_shapes=[pltpu.VMEM((tm + POOL_STATE + 1, POOL_WIDTH), F32),
                        pltpu.VMEM((GLA_HEADS, GLA_HK, GLA_HV), F32),
                        pltpu.VMEM((tm, GLA_KDIM), F32)],
        compiler_params=_params("arbitrary", "arbitrary"),
        name="mixer",
    )(u, z, alr, mem_k, mem_v, pool_hist, s0, wpool, spool, wgg, bgg, ghead)


def _merge_kernel(x_ref, br_ref, gpre_ref, wg0_ref, wg1_ref, wg2_ref, bg0_ref, bg1_ref, bg2_ref,
                  wbp_ref, wbg_ref, wbm_ref, wo_ref, gpost_ref, o_ref, h_scr):
    c = pl.program_id(1)
    last_c = pl.num_programs(1) - 1

    @pl.when(c == 0)
    def _():
        h_scr[...] = _rms(x_ref[...], gpre_ref[...]).astype(BF16)

    h = h_scr[...]
    g0 = _sigmoid(_dot(h, wg0_ref[...]) + bg0_ref[...])
    merged = g0 * _dot(br_ref[:, BR_POOL:BR_GLA], wbp_ref[...])
    g1 = _sigmoid(_dot(h, wg1_ref[...]) + bg1_ref[...])
    merged = merged + g1 * _dot(br_ref[:, BR_GLA:BR_MEM], wbg_ref[...])
    g2 = _sigmoid(_dot(h, wg2_ref[...]) + bg2_ref[...])
    merged = merged + g2 * _dot(br_ref[:, BR_MEM:D_MODEL], wbm_ref[...])
    contrib = _dot(merged.astype(BF16), wo_ref[...])

    @pl.when(c == 0)
    def _():
        o_ref[...] = contrib

    @pl.when(c > 0)
    def _():
        o_ref[...] += contrib

    @pl.when(c == last_c)
    def _():
        o_ref[...] = x_ref[...] + _rms(o_ref[...], gpost_ref[...])


def _merge(x, br, gpre, w_gate, b_gate, wbp, wbg, wbm, wo, gpost):
    n = x.shape[0]
    tm, tc = 512, 512
    nc = D_MODEL // tc
    tok = lambda i, c: (i, 0)
    return pl.pallas_call(
        _merge_kernel,
        grid=(n // tm, nc),
        in_specs=[pl.BlockSpec((tm, D_MODEL), tok),
                  pl.BlockSpec((tm, D_MODEL), tok),
                  _resident((1, D_MODEL)),
                  pl.BlockSpec((D_MODEL, tc), lambda i, c: (0, c)),
                  pl.BlockSpec((D_MODEL, tc), lambda i, c: (0, nc + c)),
                  pl.BlockSpec((D_MODEL, tc), lambda i, c: (0, 2 * nc + c)),
                  pl.BlockSpec((1, tc), lambda i, c: (0, c)),
                  pl.BlockSpec((1, tc), lambda i, c: (0, nc + c)),
                  pl.BlockSpec((1, tc), lambda i, c: (0, 2 * nc + c)),
                  pl.BlockSpec((POOL_WIDTH, tc), lambda i, c: (0, c)),
                  pl.BlockSpec((GLA_VDIM, tc), lambda i, c: (0, c)),
                  pl.BlockSpec((MEM_WIDTH, tc), lambda i, c: (0, c)),
                  pl.BlockSpec((tc, D_MODEL), lambda i, c: (c, 0)),
                  _resident((1, D_MODEL))],
        out_specs=pl.BlockSpec((tm, D_MODEL), tok),
        out_shape=jax.ShapeDtypeStruct((n, D_MODEL), F32),
        scratch_shapes=[pltpu.VMEM((tm, D_MODEL), BF16)],
        compiler_params=_params("arbitrary", "arbitrary"),
        name="merge",
    )(x, br, gpre, w_gate, w_gate, w_gate, b_gate, b_gate, b_gate, wbp, wbg, wbm, wo, gpost)


_GELU_C = 0.7978845608028654
_CONV_PAD = SUBLANES


def _gelu_tanh(x):
    return x * (0.5 * (1.0 + jnp.tanh(_GELU_C * (x + 0.044715 * (x * x * x)))))


def _ffn_kernel(*refs, tm, tf, nseq, tiles_per_seq, cached_hist):
    if cached_hist:
        (x_ref, gpre_ref, wg_ref, wv_ref, dwg_ref, dwv_ref, bg_ref, bv_ref, wd_ref, gpost_ref,
         hg_ref, hv_ref, o_ref, cg_ref, cv_ref, xn_scr, ag_buf, av_buf) = refs
    else:
        (x_ref, gpre_ref, wg_ref, wv_ref, dwg_ref, dwv_ref, bg_ref, bv_ref, wd_ref, gpost_ref,
         o_ref, cg_ref, cv_ref, xn_scr, ag_buf, av_buf, carry_g, carry_v) = refs
    i = pl.program_id(0)
    f = pl.program_id(1)
    last_f = pl.num_programs(1) - 1
    seq_rows = tm // nseq
    stride = seq_rows + _CONV_PAD

    @pl.when(f == 0)
    def _():
        xn_scr[...] = _rms(x_ref[...], gpre_ref[...]).astype(BF16)

    xn = xn_scr[...]

    def conv_half(w_ref, dw_ref, b_ref, buf, hist_ref, carry, out_ref):
        a = _dot(xn, w_ref[...])
        prev1, prev2 = [], []
        for s in range(nseq):
            base = s * stride + _CONV_PAD
            if cached_hist:
                hist = hist_ref[s]
            else:
                @pl.when(i % tiles_per_seq == 0)
                def _():
                    carry[f, _CONV_PAD - 2:_CONV_PAD, :] = jnp.zeros((CONV_W - 1, tf), F32)

                hist = carry[f, _CONV_PAD - 2:_CONV_PAD, :]
            buf[base - 2:base, :] = hist
            buf[base:base + seq_rows, :] = a[s * seq_rows:(s + 1) * seq_rows, :]
            prev1.append(buf[base - 1:base - 1 + seq_rows, :])
            prev2.append(buf[base - 2:base - 2 + seq_rows, :])
            new_hist = buf[base + seq_rows - 2:base + seq_rows, :]
            out_ref[s, f] = new_hist
            if not cached_hist:
                carry[f, _CONV_PAD - 2:_CONV_PAD, :] = new_hist
        p1 = prev1[0] if nseq == 1 else jnp.concatenate(prev1, axis=0)
        p2 = prev2[0] if nseq == 1 else jnp.concatenate(prev2, axis=0)
        return p2 * dw_ref[0:1, :] + p1 * dw_ref[1:2, :] + a * dw_ref[2:3, :] + b_ref[...]

    if cached_hist:
        c_gate = conv_half(wg_ref, dwg_ref, bg_ref, ag_buf, hg_ref, None, cg_ref)
        c_val = conv_half(wv_ref, dwv_ref, bv_ref, av_buf, hv_ref, None, cv_ref)
    else:
        c_gate = conv_half(wg_ref, dwg_ref, bg_ref, ag_buf, None, carry_g, cg_ref)
        c_val = conv_half(wv_ref, dwv_ref, bv_ref, av_buf, None, carry_v, cv_ref)
    act = (_gelu_tanh(c_gate) * c_val).astype(BF16)
    contrib = _dot(act, wd_ref[...])

    @pl.when(f == 0)
    def _():
        o_ref[...] = contrib

    @pl.when(f > 0)
    def _():
        o_ref[...] += contrib

    @pl.when(f == last_f)
    def _():
        o_ref[...] = x_ref[...] + _rms(o_ref[...], gpost_ref[...])


def _ffn(x, gpre, w_up, w_dw, b_dw, w_down, gpost, hist, *, tm, nseq, tiles_per_seq):
    n = x.shape[0]
    tf = 512
    nf = D_FF // tf
    cached_hist = hist is not None
    n_out_seq = (n // tm) * nseq if cached_hist else n // (tm * tiles_per_seq)
    tok = lambda i, f: (i, 0)
    if cached_hist:
        seq_blk = lambda i, f: (i, 0, f)
        seq_blk_v = lambda i, f: (i, 0, nf + f)
        out_blk = lambda i, f: (i, 0, 0, 0)
    else:
        out_blk = lambda i, f: (i // tiles_per_seq, 0, 0, 0)
    in_specs = [pl.BlockSpec((tm, D_MODEL), tok),
                _resident((1, D_MODEL)),
                pl.BlockSpec((D_MODEL, tf), lambda i, f: (0, f)),
                pl.BlockSpec((D_MODEL, tf), lambda i, f: (0, nf + f)),
                pl.BlockSpec((CONV_W, tf), lambda i, f: (0, f)),
                pl.BlockSpec((CONV_W, tf), lambda i, f: (0, nf + f)),
                pl.BlockSpec((1, tf), lambda i, f: (0, f)),
                pl.BlockSpec((1, tf), lambda i, f: (0, nf + f)),
                pl.BlockSpec((tf, D_MODEL), lambda i, f: (f, 0)),
                _resident((1, D_MODEL))]
    args = [x, gpre, w_up, w_up, w_dw, w_dw, b_dw, b_dw, w_down, gpost]
    scratch = [pltpu.VMEM((tm, D_MODEL), BF16),
               pltpu.VMEM((tm + nseq * _CONV_PAD, tf), F32),
               pltpu.VMEM((tm + nseq * _CONV_PAD, tf), F32)]
    if cached_hist:
        in_specs += [pl.BlockSpec((nseq, CONV_W - 1, tf), seq_blk),
                     pl.BlockSpec((nseq, CONV_W - 1, tf), seq_blk_v)]
        args += [hist, hist]
    else:
        scratch += [pltpu.VMEM((nf, _CONV_PAD, tf), F32)] * 2
    kern = functools.partial(_ffn_kernel, tm=tm, tf=tf, nseq=nseq, tiles_per_seq=tiles_per_seq,
                             cached_hist=cached_hist)
    y, cg, cv = pl.pallas_call(
        kern,
        grid=(n // tm, nf),
        in_specs=in_specs,
        out_specs=[pl.BlockSpec((tm, D_MODEL), tok),
                   pl.BlockSpec((nseq, nf, CONV_W - 1, tf), out_blk),
                   pl.BlockSpec((nseq, nf, CONV_W - 1, tf), out_blk)],
        out_shape=[jax.ShapeDtypeStruct((n, D_MODEL), F32),
                   jax.ShapeDtypeStruct((n_out_seq, nf, CONV_W - 1, tf), F32),
                   jax.ShapeDtypeStruct((n_out_seq, nf, CONV_W - 1, tf), F32)],
        scratch_shapes=scratch,
        compiler_params=_params("arbitrary", "arbitrary"),
        name="ffn",
    )(*args)
    unchunk = lambda c: c.transpose(0, 2, 1, 3).reshape(n_out_seq, CONV_W - 1, D_FF)
    return y, jnp.concatenate([unchunk(cg), unchunk(cv)], axis=-1)


def _layer(x, mem_k, mem_v, pool_hist, gla_s, conv_hist, wts, *, start_pos, mixer_tm, chunk, ffn_tm):
    batch, seq, _ = x.shape
    n = batch * seq
    xf = x.reshape(n, D_MODEL)
    u, z, alr = _in_proj(xf, wts["g_pre_mix"], wts["w_in_main"], wts["w_in_alr"])
    br, new_pool, new_s = _mixer(
        u, z, alr, mem_k.reshape(batch, N_MEM, MEM_WIDTH), mem_v.reshape(batch, N_MEM, MEM_WIDTH),
        pool_hist, gla_s, wts["w_pool"], wts["s_pool"], wts["w_gla_gate"], wts["b_gla_gate"],
        wts["g_gla_head"], batch=batch, seq=seq, tm=mixer_tm, chunk=chunk, start_pos=start_pos)
    x1 = _merge(xf, br, wts["g_pre_mix"], wts["w_gate"], wts["b_gate"], wts["w_b_pool"], wts["w_b_gla"],
                wts["w_b_mem"], wts["w_o"], wts["g_post_mix"])
    if seq >= ffn_tm:
        nseq, tiles_per_seq = 1, seq // ffn_tm
    else:
        nseq, tiles_per_seq = ffn_tm // seq, 1
    y, new_conv = _ffn(x1, wts["g_pre_ffn"], wts["w_up"], wts["w_dw"], wts["b_dw"], wts["w_down"],
                       wts["g_post_ffn"], conv_hist, tm=ffn_tm, nseq=nseq, tiles_per_seq=tiles_per_seq)
    return y.reshape(batch, seq, D_MODEL), new_pool, new_s, new_conv


def kernel(x_prompt, x_sample, mem_prompt, cache_pool, state_gla, cache_ffn_conv, cache_mem_k, cache_mem_v,
           g_mem, w_mem_k, w_mem_v, g_pre_mix, w_in, w_pool, s_pool, w_gla_gate, b_gla_gate, g_gla_head,
           w_gate, b_gate, w_b_pool, w_b_gla, w_b_mem, w_o, g_post_mix, g_pre_ffn, w_up, w_dw, b_dw,
           w_down, g_post_ffn):
    depth = w_in.shape[0]
    assert depth == 1, "one layer"
    l = 0
    bp = x_prompt.shape[0]
    a0 = POOL_WIDTH + 2 * GLA_KDIM + 2 * GLA_VDIM
    wts = {
        "g_pre_mix": g_pre_mix[l][None, :],
        "w_in_main": jnp.concatenate([w_in[l][:, :a0], w_in[l][:, a0 + GLA_GATE_RANK:]], axis=1).astype(BF16),
        "w_in_alr": jnp.pad(w_in[l][:, a0:a0 + GLA_GATE_RANK], ((0, 0), (0, LANES - GLA_GATE_RANK))).astype(BF16),
        "w_pool": w_pool[l].astype(BF16),
        "s_pool": s_pool[l][None, :],
        "w_gla_gate": jnp.pad(w_gla_gate[l], ((0, LANES - GLA_GATE_RANK), (0, 0))).astype(BF16),
        "b_gla_gate": b_gla_gate[l][None, :],
        "g_gla_head": g_gla_head[l].reshape(1, GLA_VDIM),
        "w_gate": w_gate[l].astype(BF16),
        "b_gate": b_gate[l][None, :],
        "w_b_pool": w_b_pool[l].astype(BF16),
        "w_b_gla": w_b_gla[l].astype(BF16),
        "w_b_mem": w_b_mem[l].astype(BF16),
        "w_o": w_o[l].astype(BF16),
        "g_post_mix": g_post_mix[l][None, :],
        "g_pre_ffn": g_pre_ffn[l][None, :],
        "w_up": w_up[l].astype(BF16),
        "w_dw": w_dw[l],
        "b_dw": b_dw[l][None, :],
        "w_down": w_down[l].astype(BF16),
        "g_post_ffn": g_post_ffn[l][None, :],
    }
    w_kv = jnp.concatenate([w_mem_k[l], w_mem_v[l]], axis=1).astype(BF16)
    mk, mv = _mem_kv(mem_prompt.reshape(bp * N_MEM, D_MODEL), g_mem[l][None, :], w_kv)
    mk = mk.reshape(bp, N_MEM, MEM_HEADS, MEM_HD)
    mv = mv.reshape(bp, N_MEM, MEM_HEADS, MEM_HD)
    zp = jnp.zeros((bp, POOL_STATE, POOL_WIDTH), F32)
    zs = jnp.zeros((bp, GLA_HEADS, GLA_HK, GLA_HV), F32)
    yp, pool_p, gla_p, conv_p = _layer(x_prompt, mk, mv, zp, zs, None, wts,
                                       start_pos=0, mixer_tm=1024, chunk=CHUNK, ffn_tm=512)
    ds = x_sample.shape[1]
    ys, pool_s, gla_s, conv_s = _layer(x_sample, cache_mem_k[l], cache_mem_v[l], cache_pool[l], state_gla[l],
                                       cache_ffn_conv[l], wts, start_pos=PAST_LEN, mixer_tm=ds,
                                       chunk=CHUNK if ds % CHUNK == 0 else ds, ffn_tm=512)
    return (yp, ys, pool_p[None], pool_s[None], gla_p[None], gla_s[None],
            conv_p[None], conv_s[None], mk[None], mv[None])
```
